```python
import jax, jax.numpy as jnp
from jax import lax
import numpy as np

D_MODEL = 2048
BATCH = 4
SEQ = 8192
DEPTH = 4
DEC_BATCH = 16
DEC_SEQ = 32
PAST_LEN = 4096

CHUNK = 64
N_MIXERS = 2
N_A = (DEPTH + 1) // 2
N_B = DEPTH // 2
ROPE_THETA = 500000.0
EPS = 1e-6
NEG = -1e30
A_HEADS = 16
Q_LORA = 512
KV_LORA = 512
A_NOPE = 128
A_ROPE = 64
A_V = 128
A_QBLOCK = 128
A_SCALE = (A_NOPE + A_ROPE) ** -0.5
B_HEADS = 32
B_KV = 8
B_HD = 64
B_ROT = B_HD // 4
WINDOW = 128
WIN_CHUNKS = WINDOW // CHUNK
B_SCALE = B_HD ** -0.5
D_FF = 5632
CONV_W = 3

kernel_name = 'streaming_mla_swa_convffn_step'


def rmsnorm(x, g):
    xf = x.astype(jnp.float32)
    y = xf * lax.rsqrt(jnp.mean(xf * xf, axis=-1, keepdims=True) + EPS)
    return (y * g.astype(jnp.float32)).astype(x.dtype)


def rope(x, pos, rot):
    half = rot // 2
    inv = ROPE_THETA ** (-jnp.arange(half, dtype=jnp.float32) / half)
    ang = pos.astype(jnp.float32)[:, None] * inv[None, :]
    cos = jnp.cos(ang)[:, None, :]
    sin = jnp.sin(ang)[:, None, :]
    xf = x.astype(jnp.float32)
    x1 = xf[..., :half]
    x2 = xf[..., half:rot]
    out = jnp.concatenate([x1 * cos - x2 * sin, x1 * sin + x2 * cos, xf[..., rot:]], axis=-1)
    return out.astype(x.dtype)


def mla_project(h, pos, w_in, g_q, w_qb, g_kv):
    b, s, _ = h.shape
    a = h @ w_in
    cq = rmsnorm(a[..., :Q_LORA], g_q)
    ckv = rmsnorm(a[..., Q_LORA:Q_LORA + KV_LORA], g_kv)
    kpe = rope(a[..., Q_LORA + KV_LORA:][:, :, None, :], pos, A_ROPE)[:, :, 0, :]
    q = (cq @ w_qb).reshape(b, s, A_HEADS, A_NOPE + A_ROPE)
    q_nope = q[..., :A_NOPE]
    q_pe = rope(q[..., A_NOPE:], pos, A_ROPE)
    return q_nope, q_pe, ckv, kpe


def mla_attend(q_nope, q_pe, q_pos, ckv, kpe, k_pos, w_uk, w_uv):
    q_lat = jnp.einsum('bqhn,rhn->bqhr', q_nope, w_uk)
    s = jnp.einsum('bqhr,bkr->bhqk', q_lat, ckv) + jnp.einsum('bqhe,bke->bhqk', q_pe, kpe)
    s = s.astype(jnp.float32) * A_SCALE
    visible = (k_pos // CHUNK)[None, :] <= (q_pos // CHUNK)[:, None]
    s = jnp.where(visible[None, None], s, NEG)
    p = jax.nn.softmax(s, axis=-1).astype(ckv.dtype)
    o_lat = jnp.einsum('bhqk,bkr->bqhr', p, ckv)
    o = jnp.einsum('bqhr,rhv->bqhv', o_lat, w_uv)
    return o.reshape(o.shape[0], o.shape[1], A_HEADS * A_V)


def mla_mixer(h, pos, past_ckv, past_kpe, w_in, g_q, w_qb, g_kv, w_uk, w_uv, w_o):
    b, s, _ = h.shape
    q_nope, q_pe, ckv, kpe = mla_project(h, pos, w_in, g_q, w_qb, g_kv)
    if past_ckv is None:
        nb = s // A_QBLOCK
        qn_b = q_nope.reshape(b, nb, A_QBLOCK, A_HEADS, A_NOPE).transpose(1, 0, 2, 3, 4)
        qp_b = q_pe.reshape(b, nb, A_QBLOCK, A_HEADS, A_ROPE).transpose(1, 0, 2, 3, 4)
        pos_b = pos.reshape(nb, A_QBLOCK)

        def block(args):
            qn, qp, qpos = args
            return mla_attend(qn, qp, qpos, ckv, kpe, pos, w_uk, w_uv)

        o = lax.map(block, (qn_b, qp_b, pos_b))
        o = o.transpose(1, 0, 2, 3).reshape(b, s, A_HEADS * A_V)
    else:
        past = past_ckv.shape[1]
        k_ckv = jnp.concatenate([past_ckv, ckv], axis=1)
        k_kpe = jnp.concatenate([past_kpe, kpe], axis=1)
        k_pos = jnp.arange(past + s, dtype=jnp.int32)
        o = mla_attend(q_nope, q_pe, pos, k_ckv, k_kpe, k_pos, w_uk, w_uv)
    return o @ w_o, ckv, kpe


def sink_softmax(s, sinks):
    sk = sinks[:, :, None, None]
    m = jnp.maximum(jnp.max(s, axis=-1, keepdims=True), sk)
    e = jnp.exp(s - m)
    return e / (jnp.sum(e, axis=-1, keepdims=True) + jnp.exp(sk - m))


def swa_mixer(h, pos, past_k, past_v, w_qkv, sinks, w_o):
    b, s, _ = h.shape
    g = B_HEADS // B_KV
    qkv = h @ w_qkv
    q = qkv[..., :B_HEADS * B_HD].reshape(b, s, B_HEADS, B_HD)
    k = qkv[..., B_HEADS * B_HD:(B_HEADS + B_KV) * B_HD].reshape(b, s, B_KV, B_HD)
    v = qkv[..., (B_HEADS + B_KV) * B_HD:].reshape(b, s, B_KV, B_HD)
    q = rope(q, pos, B_ROT)
    k = rope(k, pos, B_ROT)
    snk = sinks.astype(jnp.float32).reshape(B_KV, g)
    if past_k is None:
        nc = s // CHUNK
        pad = WIN_CHUNKS * CHUNK
        kp = jnp.pad(k, ((0, 0), (pad, 0), (0, 0), (0, 0))).reshape(b, nc + WIN_CHUNKS, CHUNK, B_KV, B_HD)
        vp = jnp.pad(v, ((0, 0), (pad, 0), (0, 0), (0, 0))).reshape(b, nc + WIN_CHUNKS, CHUNK, B_KV, B_HD)
        band_k = jnp.concatenate([kp[:, j:j + nc] for j in range(WIN_CHUNKS + 1)], axis=2)
        band_v = jnp.concatenate([vp[:, j:j + nc] for j in range(WIN_CHUNKS + 1)], axis=2)
        key_chunk = (jnp.arange(nc)[:, None]
                     + (jnp.arange((WIN_CHUNKS + 1) * CHUNK) // CHUNK)[None, :] - WIN_CHUNKS)
        visible = key_chunk >= 0
        qc = q.reshape(b, nc, CHUNK, B_KV, g, B_HD)
        sc = jnp.einsum('bcqkgd,bcskd->bckgqs', qc, band_k).astype(jnp.float32) * B_SCALE
        sc = jnp.where(visible[None, :, None, None, None, :], sc, NEG)
        p = sink_softmax(sc, snk).astype(v.dtype)
        o = jnp.einsum('bckgqs,bcskd->bcqkgd', p, band_v).reshape(b, s, B_HEADS * B_HD)
        kw = min(WINDOW, s)
        new_k = k[:, s - kw:]
        new_v = v[:, s - kw:]
    else:
        w = past_k.shape[1]
        k_all = jnp.concatenate([past_k, k], axis=1)
        v_all = jnp.concatenate([past_v, v], axis=1)
        k_pos = pos[0] - w + jnp.arange(w + s, dtype=jnp.int32)
        qch = pos // CHUNK
        kch = k_pos // CHUNK
        visible = (kch[None, :] <= qch[:, None]) & (kch[None, :] >= qch[:, None] - WIN_CHUNKS)
        qg = q.reshape(b, s, B_KV, g, B_HD)
        sc = jnp.einsum('bqkgd,bskd->bkgqs', qg, k_all).astype(jnp.float32) * B_SCALE
        sc = jnp.where(visible[None, None, None], sc, NEG)
        p = sink_softmax(sc, snk).astype(v.dtype)
        o = jnp.einsum('bkgqs,bskd->bqkgd', p, v_all).reshape(b, s, B_HEADS * B_HD)
        new_k = k_all[:, -w:]
        new_v = v_all[:, -w:]
    return o @ w_o, new_k, new_v


def conv_ffn(h, prev, w_in, conv_w, conv_b, w_down):
    b, s, _ = h.shape
    gu = h @ w_in
    gate = gu[..., :D_FF]
    up = gu[..., D_FF:]
    if prev is None:
        prev = jnp.zeros((b, CONV_W - 1, D_FF), gate.dtype)
    gp = jnp.concatenate([prev, gate], axis=1)
    conv = conv_b + conv_w[0] * gp[:, 0:s]
    for j in range(1, CONV_W):
        conv = conv + conv_w[j] * gp[:, j:j + s]
    y = (jax.nn.silu(conv) * up) @ w_down
    return y, gp[:, -(CONV_W - 1):]


def setup_inputs(seed: int = 0) -> dict:
    key = jax.random.key(seed)
    ks = jax.random.split(key, 24)
    f32 = jnp.float32

    def nrm(k, shape, fan_in):
        return jax.random.normal(k, shape, f32) * fan_in ** -0.5

    w_c = min(WINDOW, PAST_LEN)
    return {
        'x_prompt': jax.random.normal(ks[0], (BATCH, SEQ, D_MODEL), f32),
        'x_sample': jax.random.normal(ks[1], (DEC_BATCH, DEC_SEQ, D_MODEL), f32),
        'cache_ckv': jax.random.normal(ks[2], (N_A, DEC_BATCH, PAST_LEN, KV_LORA), f32),
        'cache_kpe': jax.random.normal(ks[3], (N_A, DEC_BATCH, PAST_LEN, A_ROPE), f32),
        'cache_win_k': jax.random.normal(ks[4], (N_B, DEC_BATCH, w_c, B_KV, B_HD), f32),
        'cache_win_v': jax.random.normal(ks[5], (N_B, DEC_BATCH, w_c, B_KV, B_HD), f32),
        'state_ffn_conv': jax.random.normal(ks[6], (DEPTH, DEC_BATCH, CONV_W - 1, D_FF), f32),
        'norm_g': 1.0 + 0.05 * jax.random.normal(ks[7], (DEPTH, 4, D_MODEL), f32),
        'mla_w_in': nrm(ks[8], (N_A, D_MODEL, Q_LORA + KV_LORA + A_ROPE), D_MODEL),
        'mla_g_q': 1.0 + 0.05 * jax.random.normal(ks[9], (N_A, Q_LORA), f32),
        'mla_w_qb': nrm(ks[10], (N_A, Q_LORA, A_HEADS * (A_NOPE + A_ROPE)), Q_LORA),
        'mla_g_kv': 1.0 + 0.05 * jax.random.normal(ks[11], (N_A, KV_LORA), f32),
        'mla_w_uk': nrm(ks[12], (N_A, KV_LORA, A_HEADS, A_NOPE), KV_LORA),
        'mla_w_uv': nrm(ks[13], (N_A, KV_LORA, A_HEADS, A_V), KV_LORA),
        'mla_w_o': nrm(ks[14], (N_A, A_HEADS * A_V, D_MODEL), A_HEADS * A_V),
        'swa_w_qkv': nrm(ks[15], (N_B, D_MODEL, (B_HEADS + 2 * B_KV) * B_HD), D_MODEL),
        'swa_sinks': 0.5 * jax.random.normal(ks[16], (N_B, B_HEADS), f32),
        'swa_w_o': nrm(ks[17], (N_B, B_HEADS * B_HD, D_MODEL), B_HEADS * B_HD),
        'ffn_w_in': nrm(ks[18], (DEPTH, D_MODEL, 2 * D_FF), D_MODEL),
        'ffn_conv_w': nrm(ks[19], (DEPTH, CONV_W, D_FF), CONV_W),
        'ffn_conv_b': 0.02 * jax.random.normal(ks[20], (DEPTH, D_FF), f32),
        'ffn_w_down': nrm(ks[21], (DEPTH, D_FF, D_MODEL), D_FF),
    }


def reference(x_prompt, x_sample, cache_ckv, cache_kpe, cache_win_k, cache_win_v, state_ffn_conv,
              norm_g, mla_w_in, mla_g_q, mla_w_qb, mla_g_kv, mla_w_uk, mla_w_uv, mla_w_o,
              swa_w_qkv, swa_sinks, swa_w_o, ffn_w_in, ffn_conv_w, ffn_conv_b, ffn_w_down):
    xp = x_prompt
    xs = x_sample
    past_len = cache_ckv.shape[2]
    pos_p = jnp.arange(xp.shape[1], dtype=jnp.int32)
    pos_s = past_len + jnp.arange(xs.shape[1], dtype=jnp.int32)
    ckv_p, kpe_p, wk_p, wv_p, fc_p = [], [], [], [], []
    ckv_s, kpe_s, wk_s, wv_s, fc_s = [], [], [], [], []
    for i in range(DEPTH):
        g = norm_g[i]
        if i % N_MIXERS == 0:
            a = i // N_MIXERS
            wts = (mla_w_in[a], mla_g_q[a], mla_w_qb[a], mla_g_kv[a], mla_w_uk[a], mla_w_uv[a], mla_w_o[a])
            mp, c_p, k_p = mla_mixer(rmsnorm(xp, g[0]), pos_p, None, None, *wts)
            ms, c_s, k_s = mla_mixer(rmsnorm(xs, g[0]), pos_s, cache_ckv[a], cache_kpe[a], *wts)
            ckv_p.append(c_p)
            kpe_p.append(k_p)
            ckv_s.append(c_s)
            kpe_s.append(k_s)
        else:
            j = i // N_MIXERS
            wts = (swa_w_qkv[j], swa_sinks[j], swa_w_o[j])
            mp, nk_p, nv_p = swa_mixer(rmsnorm(xp, g[0]), pos_p, None, None, *wts)
            ms, nk_s, nv_s = swa_mixer(rmsnorm(xs, g[0]), pos_s, cache_win_k[j], cache_win_v[j], *wts)
            wk_p.append(nk_p)
            wv_p.append(nv_p)
            wk_s.append(nk_s)
            wv_s.append(nv_s)
        xp = xp + rmsnorm(mp, g[1])
        xs = xs + rmsnorm(ms, g[1])
        fw = (ffn_w_in[i], ffn_conv_w[i], ffn_conv_b[i], ffn_w_down[i])
        fp, cp = conv_ffn(rmsnorm(xp, g[2]), None, *fw)
        fs, cs = conv_ffn(rmsnorm(xs, g[2]), state_ffn_conv[i], *fw)
        fc_p.append(cp)
        fc_s.append(cs)
        xp = xp + rmsnorm(fp, g[3])
        xs = xs + rmsnorm(fs, g[3])
    return (xp, xs,
            jnp.stack(ckv_p), jnp.stack(kpe_p), jnp.stack(wk_p), jnp.stack(wv_p), jnp.stack(fc_p),
            jnp.stack(ckv_s), jnp.stack(kpe_s), jnp.stack(wk_s), jnp.stack(wv_s), jnp.stack(fc_s))
```

```python
import functools

import jax
import jax.numpy as jnp
from jax import lax
from jax.experimental import pallas as pl
from jax.experimental.pallas import tpu as pltpu

F32 = jnp.float32
BF16 = jnp.bfloat16

CHUNK = 64
ROPE_THETA = 500000.0
EPS = 1e-6
NEG = -1e30
A_NOPE = 128
A_ROPE = 64
A_V = 128
B_HEADS = 32
B_KV = 8
B_HD = 64
B_ROT = B_HD // 4
WIN_CHUNKS = 2
CONV_W = 3

LANES = 128
VMEM_LIMIT = 56 * 1024 * 1024


def _params(*sem):
    return pltpu.CompilerParams(dimension_semantics=sem, vmem_limit_bytes=VMEM_LIMIT)


def _resident(shape):
    nd = len(shape)
    return pl.BlockSpec(shape, lambda *_: (0,) * nd, pipeline_mode=pl.Buffered(1))


def _rms(x, g):
    return x * lax.rsqrt(jnp.mean(x * x, axis=-1, keepdims=True) + EPS) * g


def _dot(a, b):
    return jnp.dot(a, b, preferred_element_type=F32)


def _dot_t(a, b):
    return lax.dot_general(a, b, (((1,), (1,)), ((), ())), preferred_element_type=F32)


def _mla_in_kernel(x_ref, g0_ref, w_ref, gq_ref, gkv_ref, tab_ref,
                   cq_ref, ckv_ref, ckvb_ref, kpe_ref, kpeb_ref, *, ql, kl):
    h = _rms(x_ref[...], g0_ref[...]).astype(BF16)
    a = _dot(h, w_ref[...])
    cq_ref[...] = _rms(a[:, :ql], gq_ref[...]).astype(BF16)
    ckv = _rms(a[:, ql:ql + kl], gkv_ref[...])
    ckv_ref[...] = ckv
    ckvb_ref[...] = ckv.astype(BF16)
    prod = a[:, ql + kl:] * tab_ref[...]
    kpe2 = prod + pltpu.roll(prod, A_ROPE, axis=1)
    kpe_ref[...] = kpe2[:, :A_ROPE]
    lane = lax.broadcasted_iota(jnp.int32, kpe2.shape, 1)
    kpeb_ref[...] = jnp.where(lane < A_ROPE, kpe2, 0.0).astype(BF16)


def _mla_in(x, g0, w, gq, gkv, tab, *, tm):
    r, d = x.shape
    ql, kl = gq.shape[1], gkv.shape[1]
    nt = tab.shape[0] // tm
    row = lambda i: (i, 0)
    return pl.pallas_call(
        functools.partial(_mla_in_kernel, ql=ql, kl=kl),
        grid=(r // tm,),
        in_specs=[pl.BlockSpec((tm, d), row), _resident(g0.shape), _resident(w.shape),
                  _resident(gq.shape), _resident(gkv.shape),
                  pl.BlockSpec((tm, LANES), lambda i: (i % nt, 0))],
        out_specs=[pl.BlockSpec((tm, ql), row), pl.BlockSpec((tm, kl), row), pl.BlockSpec((tm, kl), row),
                   pl.BlockSpec((tm, A_ROPE), row), pl.BlockSpec((tm, LANES), row)],
        out_shape=[jax.ShapeDtypeStruct((r, ql), BF16), jax.ShapeDtypeStruct((r, kl), F32),
                   jax.ShapeDtypeStruct((r, kl), BF16), jax.ShapeDtypeStruct((r, A_ROPE), F32),
                   jax.ShapeDtypeStruct((r, LANES), BF16)],
        compiler_params=_params("parallel"),
        name="mla_in",
    )(x, g0, w, gq, gkv, tab)


def _mla_q_kernel(cq_ref, w_ref, tab_ref, qn_ref, qp_ref, *, hn, scale):
    a = _dot(cq_ref[...], w_ref[...])
    qn_ref[...] = (a[:, :hn] * scale).astype(BF16)
    reps = hn // LANES
    c = jnp.concatenate([tab_ref[:, :LANES]] * reps, axis=1)
    s = jnp.concatenate([tab_ref[:, LANES:]] * reps, axis=1)
    qp_ref[...] = ((a[:, hn:2 * hn] * c + a[:, 2 * hn:] * s) * scale).astype(BF16)


def _mla_q(cq, w, tab, *, tm, scale):
    r, ql = cq.shape
    hn = w.shape[1] // 3
    nt = tab.shape[0] // tm
    row = lambda i: (i, 0)
    return pl.pallas_call(
        functools.partial(_mla_q_kernel, hn=hn, scale=scale),
        grid=(r // tm,),
        in_specs=[pl.BlockSpec((tm, ql), row), _resident(w.shape),
                  pl.BlockSpec((tm, 2 * LANES), lambda i: (i % nt, 0))],
        out_specs=[pl.BlockSpec((tm, hn), row), pl.BlockSpec((tm, hn), row)],
        out_shape=[jax.ShapeDtypeStruct((r, hn), BF16), jax.ShapeDtypeStruct((r, hn), BF16)],
        compiler_params=_params("parallel"),
        name="mla_q",
    )(cq, w, tab)


def _mla_kv_kernel(c_ref, w_ref, k_ref, v_ref, *, hn):
    a = _dot(c_ref[...], w_ref[...])
    k_ref[...] = a[:, :hn].astype(BF16)
    v_ref[...] = a[:, hn:].astype(BF16)


def _mla_kv(ckvb, w, *, tm):
    r, kl = ckvb.shape
    hn = w.shape[1] // 2
    row = lambda i: (i, 0)
    return pl.pallas_call(
        functools.partial(_mla_kv_kernel, hn=hn),
        grid=(r // tm,),
        in_specs=[pl.BlockSpec((tm, kl), row), _resident(w.shape)],
        out_specs=[pl.BlockSpec((tm, hn), row), pl.BlockSpec((tm, hn), row)],
        out_shape=[jax.ShapeDtypeStruct((r, hn), BF16), jax.ShapeDtypeStruct((r, hn), BF16)],
        compiler_params=_params("parallel"),
        name="mla_kv",
    )(ckvb, w)


def _mla_attn_kernel(qn_ref, qp_ref, kn_ref, kp_ref, v_ref, o_ref, *, t):
    i = pl.program_id(2)
    q = jnp.concatenate([qn_ref[...], qp_ref[...]], axis=1)

    def step(j, carry, masked):
        m, l, acc = carry
        ks = pl.ds(pl.multiple_of(j * t, t), t)
        k = jnp.concatenate([kn_ref[ks, :], kp_ref[ks, :]], axis=1)
        s = _dot_t(q, k)
        if masked:
            rc = lax.broadcasted_iota(jnp.int32, s.shape, 0) // CHUNK
            kc = lax.broadcasted_iota(jnp.int32, s.shape, 1) // CHUNK
            s = jnp.where(kc <= rc, s, NEG)
        m_new = jnp.maximum(m, jnp.max(s, axis=1, keepdims=True))
        alpha = jnp.exp(m - m_new)
        p = jnp.exp(s - m_new)
        l = alpha * l + jnp.sum(p, axis=1, keepdims=True)
        acc = alpha * acc + _dot(p.astype(BF16), v_ref[ks, :])
        return m_new, l, acc

    init = (jnp.full((t, 1), NEG, F32), jnp.zeros((t, 1), F32), jnp.zeros((t, A_V), F32))
    carry = lax.fori_loop(0, i, lambda j, c: step(j, c, False), init)
    _, l, acc = step(i, carry, True)
    o_ref[...] = (acc / l).astype(BF16)


def _mla_attn(qn, qp, kn, kpb, v, *, nb, s, heads, t):
    r = qn.shape[0]
    nq = s // t
    qmap = lambda b, h, i: (b * nq + i, h)
    kmap = lambda b, h, i: (b, h)
    return pl.pallas_call(
        functools.partial(_mla_attn_kernel, t=t),
        grid=(nb, heads, nq),
        in_specs=[pl.BlockSpec((t, A_NOPE), qmap), pl.BlockSpec((t, LANES), qmap),
                  pl.BlockSpec((s, A_NOPE), kmap), pl.BlockSpec((s, LANES), lambda b, h, i: (b, 0)),
                  pl.BlockSpec((s, A_V), kmap)],
        out_specs=pl.BlockSpec((t, A_V), qmap),
        out_shape=jax.ShapeDtypeStruct((r, heads * A_V), BF16),
        compiler_params=_params("parallel", "parallel", "arbitrary"),
        name="mla_attn",
    )(qn, qp, kn, kpb, v)


def _mla_dec_kernel(qn_ref, qp_ref, wuk_ref, wuv_ref, cc_ref, ck_ref, nc_ref, nk_ref, o_ref,
                    ql_s, qpe_s, m_s, l_s, acc_s, *, heads, sq, tk, past):
    j = pl.program_id(1)
    rows = heads * sq

    def online(s, vals):
        m_old = m_s[...]
        m_new = jnp.maximum(m_old, jnp.max(s, axis=1, keepdims=True))
        alpha = jnp.exp(m_old - m_new)
        p = jnp.exp(s - m_new)
        l_s[...] = alpha * l_s[...] + jnp.sum(p, axis=1, keepdims=True)
        acc_s[...] = alpha * acc_s[...] + _dot(p.astype(BF16), vals)
        m_s[...] = m_new

    def visible(s, k0):
        qc = (past + lax.broadcasted_iota(jnp.int32, s.shape, 0) % sq) // CHUNK
        kc = (k0 + lax.broadcasted_iota(jnp.int32, s.shape, 1)) // CHUNK
        return jnp.where(kc <= qc, s, NEG)

    @pl.when(j == 0)
    def _():
        for h in range(heads):
            ql_s[h * sq:(h + 1) * sq, :] = _dot(qn_ref[:, h * A_NOPE:(h + 1) * A_NOPE], wuk_ref[h]).astype(BF16)
            qpe_s[h * sq:(h + 1) * sq, :] = qp_ref[:, h * LANES:(h + 1) * LANES]
        m_s[...] = jnp.full(m_s.shape, NEG, F32)
        l_s[...] = jnp.zeros(l_s.shape, F32)
        acc_s[...] = jnp.zeros(acc_s.shape, F32)
        nc = nc_ref[...]
        s = _dot_t(ql_s[...], nc) + _dot_t(qpe_s[...], nk_ref[...])
        online(visible(s, past), nc)

    cc = cc_ref[0].astype(BF16)
    s = _dot_t(ql_s[...], cc) + _dot_t(qpe_s[...], ck_ref[0])
    online(visible(s, j * tk), cc)

    @pl.when(j == pl.num_programs(1) - 1)
    def _():
        ol = (acc_s[...] / l_s[...]).astype(BF16)
        for h in range(heads):
            o_ref[:, h * A_V:(h + 1) * A_V] = _dot(ol[h * sq:(h + 1) * sq, :], wuv_ref[h]).astype(BF16)


def _mla_dec(qn, qp, wukt, wuv, cache_c, cache_kp, ckvb, kpeb, *, sq, tk):
    nb, past, kl = cache_c.shape
    heads = wukt.shape[0]
    rows = heads * sq
    seq = lambda b, j: (b, 0)
    return pl.pallas_call(
        functools.partial(_mla_dec_kernel, heads=heads, sq=sq, tk=tk, past=past),
        grid=(nb, past // tk),
        in_specs=[pl.BlockSpec((sq, heads * A_NOPE), seq), pl.BlockSpec((sq, heads * LANES), seq),
                  _resident(wukt.shape), _resident(wuv.shape),
                  pl.BlockSpec((1, tk, kl), lambda b, j: (b, j, 0)),
                  pl.BlockSpec((1, tk, LANES), lambda b, j: (b, j, 0)),
                  pl.BlockSpec((sq, kl), seq), pl.BlockSpec((sq, LANES), seq)],
        out_specs=pl.BlockSpec((sq, heads * A_V), seq),
        out_shape=jax.ShapeDtypeStruct((nb * sq, heads * A_V), BF16),
        scratch_shapes=[pltpu.VMEM((rows, kl), BF16), pltpu.VMEM((rows, LANES), BF16),
                        pltpu.VMEM((rows, 1), F32), pltpu.VMEM((rows, 1), F32), pltpu.VMEM((rows, kl), F32)],
        compiler_params=_params("parallel", "arbitrary"),
        name="mla_dec",
    )(qn, qp, wukt, wuv, cache_c, cache_kp, ckvb, kpeb)


def _swa_in_kernel(x_ref, g0_ref, w_ref, tab_ref, q_ref, kd_ref, vd_ref, kdb_ref, vdb_ref, *, nq, nk, scale):
    h = _rms(x_ref[...], g0_ref[...]).astype(BF16)
    a = _dot(h, w_ref[...])
    qk = a[:, :nq + nk]
    n = nq + nk
    reps = n // LANES
    c = jnp.concatenate([tab_ref[:, :LANES]] * reps, axis=1)
    s = jnp.concatenate([tab_ref[:, LANES:]] * reps, axis=1)
    half = B_ROT // 2
    lane = lax.broadcasted_iota(jnp.int32, qk.shape, 1) % B_HD
    swapped = jnp.where(lane < half, pltpu.roll(qk, n - half, axis=1), pltpu.roll(qk, half, axis=1))
    roped = qk * c + swapped * s
    q_ref[...] = (roped[:, :nq] * scale).astype(BF16)
    kd = roped[:, nq:]
    vd = a[:, n:]
    kd_ref[...] = kd
    vd_ref[...] = vd
    kdb_ref[...] = kd.astype(BF16)
    vdb_ref[...] = vd.astype(BF16)


def _swa_in(x, g0, w, tab, *, tm, scale):
    r, d = x.shape
    nq = B_HEADS * B_HD
    nk = 2 * B_KV * B_HD
    nt = tab.shape[0] // tm
    row = lambda i: (i, 0)
    return pl.pallas_call(
        functools.partial(_swa_in_kernel, nq=nq, nk=nk, scale=scale),
        grid=(r // tm,),
        in_specs=[pl.BlockSpec((tm, d), row), _resident(g0.shape), _resident(w.shape),
                  pl.BlockSpec((tm, 2 * LANES), lambda i: (i % nt, 0))],
        out_specs=[pl.BlockSpec((tm, nq), row), pl.BlockSpec((tm, nk), row), pl.BlockSpec((tm, nk), row),
                   pl.BlockSpec((tm, nk), row), pl.BlockSpec((tm, nk), row)],
        out_shape=[jax.ShapeDtypeStruct((r, nq), BF16), jax.ShapeDtypeStruct((r, nk), F32),
                   jax.ShapeDtypeStruct((r, nk), F32), jax.ShapeDtypeStruct((r, nk), BF16),
                   jax.ShapeDtypeStruct((r, nk), BF16)],
        compiler_params=_params("parallel"),
        name="swa_in",
    )(x, g0, w, tab)


def _swa_group(q_ref, o_ref, sink_ref, g, k2, v2, mask, tq):
    gq = B_HEADS // B_KV
    lane = lax.broadcasted_iota(jnp.int32, (tq, LANES), 1)
    lo = (lane < B_HD).astype(BF16)
    hi = (lane >= B_HD).astype(BF16)
    base = g * gq * B_HD
    pairs = [q_ref[:, base + p * LANES: base + (p + 1) * LANES] for p in range(gq // 2)]
    lhs = jnp.concatenate([x * sel for x in pairs for sel in (lo, hi)], axis=0)
    s = jnp.where(mask, _dot_t(lhs, k2), NEG)
    sk = jnp.concatenate([jnp.full((tq, 1), sink_ref[g * gq + t], F32) for t in range(gq)], axis=0)
    m = jnp.maximum(jnp.max(s, axis=1, keepdims=True), sk)
    e = jnp.exp(s - m)
    den = jnp.sum(e, axis=1, keepdims=True) + jnp.exp(sk - m)
    o = _dot(e.astype(BF16), v2) / den
    lanef = lane < B_HD
    for p in range(gq // 2):
        pair = jnp.where(lanef, o[2 * p * tq:(2 * p + 1) * tq], o[(2 * p + 1) * tq:(2 * p + 2) * tq])
        o_ref[:, base + p * LANES: base + (p + 1) * LANES] = pair.astype(BF16)


def _swa_attn_kernel(sink_ref, q_ref, kp_ref, kc_ref, vp_ref, vc_ref, o_ref, *, tq):
    i = pl.program_id(1)
    gq = B_HEADS // B_KV
    shape = (gq * tq, 2 * tq)
    qc = (lax.broadcasted_iota(jnp.int32, shape, 0) % tq) // CHUNK
    kc = lax.broadcasted_iota(jnp.int32, shape, 1) // CHUNK - tq // CHUNK
    mask = (kc <= qc) & (kc >= qc - WIN_CHUNKS) & ((i > 0) | (kc >= 0))
    for g in range(B_KV):
        cols = slice(g * LANES, (g + 1) * LANES)
        k2 = jnp.concatenate([kp_ref[:, cols], kc_ref[:, cols]], axis=0)
        v2 = jnp.concatenate([vp_ref[:, cols], vc_ref[:, cols]], axis=0)
        _swa_group(q_ref, o_ref, sink_ref, g, k2, v2, mask, tq)


def _swa_attn(sinks, q, kdb, vdb, *, nb, s):
    tq = WIN_CHUNKS * CHUNK
    r, nq = q.shape
    nk = kdb.shape[1]
    nblk = s // tq
    cur = lambda b, i: (b * nblk + i, 0)
    prev = lambda b, i: (b * nblk + jnp.maximum(i - 1, 0), 0)
    return pl.pallas_call(
        functools.partial(_swa_attn_kernel, tq=tq),
        grid=(nb, nblk),
        in_specs=[pl.BlockSpec(memory_space=pltpu.SMEM),
                  pl.BlockSpec((tq, nq), cur), pl.BlockSpec((tq, nk), prev), pl.BlockSpec((tq, nk), cur),
                  pl.BlockSpec((tq, nk), prev), pl.BlockSpec((tq, nk), cur)],
        out_specs=pl.BlockSpec((tq, nq), cur),
        out_shape=jax.ShapeDtypeStruct((r, nq), BF16),
        compiler_params=_params("parallel", "parallel"),
        name="swa_attn",
    )(sinks, q, kdb, kdb, vdb, vdb)


def _swa_dec_kernel(sink_ref, q_ref, k_ref, v_ref, o_ref, *, sq, nkeys, kpad, past):
    gq = B_HEADS // B_KV
    shape = (gq * sq, kpad)
    qc = (past + lax.broadcasted_iota(jnp.int32, shape, 0) % sq) // CHUNK
    col = lax.broadcasted_iota(jnp.int32, shape, 1)
    kc = (past + sq - nkeys + col) // CHUNK
    mask = (kc <= qc) & (kc >= qc - WIN_CHUNKS) & (col < nkeys)
    for g in range(B_KV):
        cols = slice(g * LANES, (g + 1) * LANES)
        _swa_group(q_ref, o_ref, sink_ref, g, k_ref[0, :, cols], v_ref[0, :, cols], mask, sq)


def _swa_dec(sinks, q, k_all, v_all, *, sq, nkeys, past):
    nb, kpad, nk = k_all.shape
    nq = q.shape[1]
    return pl.pallas_call(
        functools.partial(_swa_dec_kernel, sq=sq, nkeys=nkeys, kpad=kpad, past=past),
        grid=(nb,),
        in_specs=[pl.BlockSpec(memory_space=pltpu.SMEM),
                  pl.BlockSpec((sq, nq), lambda b: (b, 0)),
                  pl.BlockSpec((1, kpad, nk), lambda b: (b, 0, 0)),
                  pl.BlockSpec((1, kpad, nk), lambda b: (b, 0, 0))],
        out_specs=pl.BlockSpec((sq, nq), lambda b: (b, 0)),
        out_shape=jax.ShapeDtypeStruct((nb * sq, nq), BF16),
        compiler_params=_params("parallel"),
        name="swa_dec",
    )(sinks, q, k_all, v_all)


def _mix_out_kernel(o_ref, w_ref, x_ref, g1_ref, g2_ref, x1_ref, h2_ref):
    mp = _dot(o_ref[...], w_ref[...])
    x1 = x_ref[...] + _rms(mp, g1_ref[...])
    x1_ref[...] = x1
    h2_ref[...] = _rms(x1, g2_ref[...]).astype(BF16)


def _mix_out(o, w, x, g1, g2, *, tm):
    r, d = x.shape
    row = lambda i: (i, 0)
    return pl.pallas_call(
        _mix_out_kernel,
        grid=(r // tm,),
        in_specs=[pl.BlockSpec((tm, o.shape[1]), row), _resident(w.shape), pl.BlockSpec((tm, d), row),
                  _resident(g1.shape), _resident(g2.shape)],
        out_specs=[pl.BlockSpec((tm, d), row), pl.BlockSpec((tm, d), row)],
        out_shape=[jax.ShapeDtypeStruct((r, d), F32), jax.ShapeDtypeStruct((r, d), BF16)],
        compiler_params=_params("parallel"),
        name="mix_out",
    )(o, w, x, g1, g2)


def _ffn_tail(conv_in, u, cw_ref, cb_ref, wd_ref, x_ref, g3_ref, o_ref, acc_ref):
    g, g1, g2 = conv_in
    j = pl.program_id(1)
    cw = cw_ref[...]
    conv = cb_ref[...] + cw[0:1] * g2 + cw[1:2] * g1 + cw[2:3] * g
    act = conv / (1.0 + jnp.exp(-conv)) * u
    contrib = _dot(act.astype(BF16), wd_ref[...])

    @pl.when(j == 0)
    def _():
        acc_ref[...] = contrib

    @pl.when(j > 0)
    def _():
        acc_ref[...] += contrib

    @pl.when(j == pl.num_programs(1) - 1)
    def _():
        o_ref[...] = x_ref[...] + _rms(acc_ref[...], g3_ref[...])


def _ffn_prompt_kernel(h_ref, x_ref, wg_ref, wu_ref, cw_ref, cb_ref, wd_ref, g3_ref,
                       o_ref, st_ref, acc_ref, carry_ref, *, blocks_per_seq):
    i = pl.program_id(0)
    j = pl.program_id(1)
    h = h_ref[...]
    tm = h.shape[0]
    g = _dot(h, wg_ref[...])
    u = _dot(h, wu_ref[...])

    @pl.when(i % blocks_per_seq == 0)
    def _():
        carry_ref[j] = jnp.zeros(carry_ref.shape[1:], F32)

    prev = carry_ref[j]
    carry_ref[j] = g[tm - 8:, :]
    st_ref[0] = g[tm - (CONV_W - 1):, :]
    row = lax.broadcasted_iota(jnp.int32, g.shape, 0)
    g1 = jnp.where(row == 0, prev[7:8, :], pltpu.roll(g, 1, axis=0))
    g2 = jnp.where(row == 0, prev[6:7, :], jnp.where(row == 1, prev[7:8, :], pltpu.roll(g, 2, axis=0)))
    _ffn_tail((g, g1, g2), u, cw_ref, cb_ref, wd_ref, x_ref, g3_ref, o_ref, acc_ref)


def _ffn_sample_kernel(h_ref, x_ref, wg_ref, wu_ref, cw_ref, cb_ref, wd_ref, g3_ref, i1_ref, i2_ref,
                       o_ref, gate_ref, acc_ref, *, sq):
    h = h_ref[...]
    g = _dot(h, wg_ref[...])
    u = _dot(h, wu_ref[...])
    gate_ref[...] = g
    pos = lax.broadcasted_iota(jnp.int32, g.shape, 0) % sq
    g1 = jnp.where(pos == 0, i1_ref[...], pltpu.roll(g, 1, axis=0))
    g2 = jnp.where(pos < 2, i2_ref[...], pltpu.roll(g, 2, axis=0))
    _ffn_tail((g, g1, g2), u, cw_ref, cb_ref, wd_ref, x_ref, g3_ref, o_ref, acc_ref)


def _ffn_specs(tm, tf, d, nf):
    row = lambda i, j: (i, 0)
    return [pl.BlockSpec((tm, d), row), pl.BlockSpec((tm, d), row),
            pl.BlockSpec((d, tf), lambda i, j: (0, j)), pl.BlockSpec((d, tf), lambda i, j: (0, j + nf)),
            pl.BlockSpec((CONV_W, tf), lambda i, j: (0, j)), pl.BlockSpec((1, tf), lambda i, j: (0, j)),
            pl.BlockSpec((tf, d), lambda i, j: (j, 0)), pl.BlockSpec((1, d), lambda i, j: (0, 0))]


def _ffn_prompt(h2, x1, w_in, cw, cb, wd, g3, *, nb, s, tm, tf):
    r, d = x1.shape
    dff = wd.shape[0]
    nf = dff // tf
    bps = s // tm
    return pl.pallas_call(
        functools.partial(_ffn_prompt_kernel, blocks_per_seq=bps),
        grid=(r // tm, nf),
        in_specs=_ffn_specs(tm, tf, d, nf),
        out_specs=[pl.BlockSpec((tm, d), lambda i, j: (i, 0)),
                   pl.BlockSpec((1, CONV_W - 1, tf), lambda i, j: (i, 0, j))],
        out_shape=[jax.ShapeDtypeStruct((r, d), F32), jax.ShapeDtypeStruct((r // tm, CONV_W - 1, dff), F32)],
        scratch_shapes=[pltpu.VMEM((tm, d), F32), pltpu.VMEM((nf, 8, tf), F32)],
        compiler_params=_params("arbitrary", "arbitrary"),
        name="ffn_prompt",
    )(h2, x1, w_in, w_in, cw, cb, wd, g3)


def _ffn_sample(h2, x1, w_in, cw, cb, wd, g3, inj1, inj2, *, sq, tm, tf):
    r, d = x1.shape
    dff = wd.shape[0]
    nf = dff // tf
    tile = lambda i, j: (i, j)
    return pl.pallas_call(
        functools.partial(_ffn_sample_kernel, sq=sq),
        grid=(r // tm, nf),
        in_specs=_ffn_specs(tm, tf, d, nf) + [pl.BlockSpec((tm, tf), tile), pl.BlockSpec((tm, tf), tile)],
        out_specs=[pl.BlockSpec((tm, d), lambda i, j: (i, 0)), pl.BlockSpec((tm, tf), tile)],
        out_shape=[jax.ShapeDtypeStruct((r, d), F32), jax.ShapeDtypeStruct((r, dff), F32)],
        scratch_shapes=[pltpu.VMEM((tm, d), F32)],
        compiler_params=_params("parallel", "arbitrary"),
        name="ffn_sample",
    )(h2, x1, w_in, w_in, cw, cb, wd, g3, inj1, inj2)


def _rope_cs(pos, rot):
    half = rot // 2
    inv = ROPE_THETA ** (-jnp.arange(half, dtype=F32) / half)
    ang = pos.astype(F32)[:, None] * inv[None, :]
    return jnp.cos(ang), jnp.sin(ang)


def _mla_tables(pos):
    c, s = _rope_cs(pos, A_ROPE)
    cc = jnp.concatenate([c, c], axis=1)
    ss = jnp.concatenate([-s, s], axis=1)
    tab_k = jnp.concatenate([cc, ss], axis=1)
    tab_q = jnp.concatenate([cc, cc, ss, ss], axis=1)
    return tab_k, tab_q


def _swa_table(pos):
    c, s = _rope_cs(pos, B_ROT)
    p = pos.shape[0]
    rest = B_HD - B_ROT
    cc = jnp.concatenate([c, c, jnp.ones((p, rest), F32)], axis=1)
    ss = jnp.concatenate([-s, s, jnp.zeros((p, rest), F32)], axis=1)
    return jnp.concatenate([cc, cc, ss, ss], axis=1)


def _swap_halves(w):
    h = w.shape[-1] // 2
    return jnp.concatenate([w[..., h:], w[..., :h]], axis=-1)


def _row_tile(r, want):
    t = min(r, want)
    assert r % t == 0
    return t


def kernel(x_prompt, x_sample, cache_ckv, cache_kpe, cache_win_k, cache_win_v, state_ffn_conv, norm_g, mla_w_in, mla_g_q, mla_w_qb, mla_g_kv, mla_w_uk, mla_w_uv, mla_w_o, swa_w_qkv, swa_sinks, swa_w_o, ffn_w_in, ffn_conv_w, ffn_conv_b, ffn_w_down):
    nb, s, d = x_prompt.shape
    db, sq, _ = x_sample.shape
    depth = norm_g.shape[0]
    past = cache_ckv.shape[2]
    win = cache_win_k.shape[2]
    ql, kl = mla_g_q.shape[1], mla_g_kv.shape[1]
    heads = mla_w_uk.shape[2]
    dff = ffn_w_down.shape[1]
    gq = B_HEADS // B_KV
    assert mla_w_uk.shape[3] == A_NOPE and mla_w_uv.shape[3] == A_V
    assert mla_w_in.shape[2] == ql + kl + A_ROPE and sq >= CONV_W - 1 and s >= WIN_CHUNKS * CHUNK
    a_scale = (A_NOPE + A_ROPE) ** -0.5
    b_scale = B_HD ** -0.5

    xp = x_prompt.reshape(nb * s, d)
    xs = x_sample.reshape(db * sq, d)
    pos_p = jnp.arange(s, dtype=jnp.int32)
    pos_s = jnp.tile(past + jnp.arange(sq, dtype=jnp.int32), db)
    tabk_p, tabq_p = _mla_tables(pos_p)
    tabk_s, tabq_s = _mla_tables(pos_s)
    tabb_p = _swa_table(pos_p)
    tabb_s = _swa_table(pos_s)

    tm_p = _row_tile(nb * s, 512)
    tm_s = _row_tile(db * sq, 512)
    tf = 512
    assert s % tm_p == 0 and dff % tf == 0
    t_attn = 256
    tk_dec = _row_tile(past, 1024)

    outs = {k: [] for k in ("ckv_p", "kpe_p", "wk_p", "wv_p", "fc_p", "ckv_s", "kpe_s", "wk_s", "wv_s", "fc_s")}
    for i in range(depth):
        g = norm_g[i][:, None, :]
        if i % 2 == 0:
            a = i // 2
            w_in = mla_w_in[a]
            w1 = jnp.concatenate([w_in, _swap_halves(w_in[:, ql + kl:])], axis=1).astype(BF16)
            wq = mla_w_qb[a].reshape(ql, heads, A_NOPE + A_ROPE)
            pad = ((0, 0), (0, 0), (0, LANES - A_ROPE))
            w2 = jnp.concatenate([wq[:, :, :A_NOPE].reshape(ql, heads * A_NOPE),
                                  jnp.pad(wq[:, :, A_NOPE:], pad).reshape(ql, heads * LANES),
                                  jnp.pad(_swap_halves(wq[:, :, A_NOPE:]), pad).reshape(ql, heads * LANES)],
                                 axis=1).astype(BF16)
            wkv = jnp.concatenate([mla_w_uk[a].reshape(kl, heads * A_NOPE),
                                   mla_w_uv[a].reshape(kl, heads * A_V)], axis=1).astype(BF16)
            wukt = mla_w_uk[a].transpose(1, 2, 0).astype(BF16)
            wuv = mla_w_uv[a].transpose(1, 0, 2).astype(BF16)
            wo = mla_w_o[a].astype(BF16)
            gq_, gkv_ = mla_g_q[a][None, :], mla_g_kv[a][None, :]

            cq, ckv, ckvb, kpe, kpeb = _mla_in(xp, g[0], w1, gq_, gkv_, tabk_p, tm=tm_p)
            qn, qp = _mla_q(cq, w2, tabq_p, tm=_row_tile(nb * s, 256), scale=a_scale)
            kn, v = _mla_kv(ckvb, wkv, tm=_row_tile(nb * s, 256))
            op = _mla_attn(qn, qp, kn, kpeb, v, nb=nb, s=s, heads=heads, t=min(t_attn, s))
            outs["ckv_p"].append(ckv.reshape(nb, s, kl))
            outs["kpe_p"].append(kpe.reshape(nb, s, A_ROPE))

            cq, ckv, ckvb, kpe, kpeb = _mla_in(xs, g[0], w1, gq_, gkv_, tabk_s, tm=tm_s)
            qn, qp = _mla_q(cq, w2, tabq_s, tm=_row_tile(db * sq, 256), scale=a_scale)
            cache_kp = jnp.pad(cache_kpe[a].astype(BF16), ((0, 0), (0, 0), (0, LANES - A_ROPE)))
            os_ = _mla_dec(qn, qp, wukt, wuv, cache_ckv[a], cache_kp, ckvb, kpeb, sq=sq, tk=tk_dec)
            outs["ckv_s"].append(ckv.reshape(db, sq, kl))
            outs["kpe_s"].append(kpe.reshape(db, sq, A_ROPE))
        else:
            jb = i // 2
            wqkv = swa_w_qkv[jb]
            nq = B_HEADS * B_HD
            nkv = B_KV * B_HD
            dup = lambda w: jnp.concatenate([w.reshape(d, B_KV, B_HD)] * 2, axis=2).reshape(d, 2 * nkv)
            w3 = jnp.concatenate([wqkv[:, :nq], dup(wqkv[:, nq:nq + nkv]), dup(wqkv[:, nq + nkv:])],
                                 axis=1).astype(BF16)
            wo = swa_w_o[jb].astype(BF16)
            sinks = swa_sinks[jb].astype(F32)
            undup = lambda t, n, rows: t.reshape(n, rows, B_KV, 2, B_HD)[:, :, :, 0, :]

            q, kd, vd, kdb, vdb = _swa_in(xp, g[0], w3, tabb_p, tm=_row_tile(nb * s, 256), scale=b_scale)
            op = _swa_attn(sinks, q, kdb, vdb, nb=nb, s=s)
            kw = min(WIN_CHUNKS * CHUNK, s)
            outs["wk_p"].append(undup(kd, nb, s)[:, s - kw:])
            outs["wv_p"].append(undup(vd, nb, s)[:, s - kw:])

            q, kd, vd, kdb, vdb = _swa_in(xs, g[0], w3, tabb_s, tm=_row_tile(db * sq, 256), scale=b_scale)
            nkeys = win + sq
            kpad = -(-nkeys // LANES) * LANES
            dup_c = lambda c: jnp.concatenate([c, c], axis=3).reshape(db, win, 2 * nkv).astype(BF16)
            cat = lambda c, new: jnp.pad(jnp.concatenate([dup_c(c), new.reshape(db, sq, 2 * nkv)], axis=1),
                                         ((0, 0), (0, kpad - nkeys), (0, 0)))
            os_ = _swa_dec(sinks, q, cat(cache_win_k[jb], kdb), cat(cache_win_v[jb], vdb),
                           sq=sq, nkeys=nkeys, past=past)
            outs["wk_s"].append(jnp.concatenate([cache_win_k[jb], undup(kd, db, sq)], axis=1)[:, -win:])
            outs["wv_s"].append(jnp.concatenate([cache_win_v[jb], undup(vd, db, sq)], axis=1)[:, -win:])

        x1p, h2p = _mix_out(op, wo, xp, g[1], g[2], tm=_row_tile(nb * s, 256))
        x1s, h2s = _mix_out(os_, wo, xs, g[1], g[2], tm=_row_tile(db * sq, 256))

        fw_in = ffn_w_in[i].astype(BF16)
        fw_d = ffn_w_down[i].astype(BF16)
        cw, cb = ffn_conv_w[i], ffn_conv_b[i][None, :]
        xp, fc = _ffn_prompt(h2p, x1p, fw_in, cw, cb, fw_d, g[3], nb=nb, s=s, tm=tm_p, tf=tf)
        outs["fc_p"].append(fc[s // tm_p - 1::s // tm_p])
        st = state_ffn_conv[i]
        zeros = jnp.zeros((db, sq, dff), F32)
        inj1 = zeros.at[:, 0].set(st[:, 1]).reshape(db * sq, dff)
        inj2 = zeros.at[:, 0].set(st[:, 0]).at[:, 1].set(st[:, 1]).reshape(db * sq, dff)
        xs, gate = _ffn_sample(h2s, x1s, fw_in, cw, cb, fw_d, g[3], inj1, inj2, sq=sq, tm=tm_s, tf=tf)
        outs["fc_s"].append(gate.reshape(db, sq, dff)[:, sq - (CONV_W - 1):])

    st = lambda k: jnp.stack(outs[k])
    return (xp.reshape(nb, s, d), xs.reshape(db, sq, d),
            st("ckv_p"), st("kpe_p"), st("wk_p"), st("wv_p"), st("fc_p"),
            st("ckv_s"), st("kpe_s"), st("wk_s"), st("wv_s"), st("fc_s"))
```

```python
import functools

import jax
import jax.numpy as jnp
from jax import lax
from jax.experimental import pallas as pl
from jax.experimental.pallas import tpu as pltpu

F32 = jnp.float32
BF16 = jnp.bfloat16

CHUNK = 64
ROPE_THETA = 500000.0
EPS = 1e-6
NEG = -1e30
A_NOPE = 128
A_ROPE = 64
A_V = 128
B_HEADS = 32
B_KV = 8
B_HD = 64
B_ROT = B_HD // 4
WIN_CHUNKS = 2
CONV_W = 3
LOG2_E = 1.4426950408889634

LANES = 128
VMEM_LIMIT = 56 * 1024 * 1024


def _params(*sem):
    return pltpu.CompilerParams(dimension_semantics=sem, vmem_limit_bytes=VMEM_LIMIT)


def _resident(shape):
    nd = len(shape)
    return pl.BlockSpec(shape, lambda *_: (0,) * nd, pipeline_mode=pl.Buffered(1))


def _rms(x, g):
    return x * lax.rsqrt(jnp.mean(x * x, axis=-1, keepdims=True) + EPS) * g


def _dot(a, b):
    return jnp.dot(a, b, preferred_element_type=F32)


def _dot_t(a, b):
    return lax.dot_general(a, b, (((1,), (1,)), ((), ())), preferred_element_type=F32)


def _mla_in_kernel(x_ref, g0_ref, w_ref, gq_ref, gkv_ref, tab_ref,
                   cq_ref, ckv_ref, ckvb_ref, kpe_ref, kpeb_ref, *, ql, kl):
    h = _rms(x_ref[...], g0_ref[...]).astype(BF16)
    a = _dot(h, w_ref[...])
    cq_ref[...] = _rms(a[:, :ql], gq_ref[...]).astype(BF16)
    ckv = _rms(a[:, ql:ql + kl], gkv_ref[...])
    ckv_ref[...] = ckv
    ckvb_ref[...] = ckv.astype(BF16)
    prod = a[:, ql + kl:] * tab_ref[...]
    kpe2 = prod + pltpu.roll(prod, A_ROPE, axis=1)
    kpe_ref[...] = kpe2[:, :A_ROPE]
    lane = lax.broadcasted_iota(jnp.int32, kpe2.shape, 1)
    kpeb_ref[...] = jnp.where(lane < A_ROPE, kpe2, 0.0).astype(BF16)


def _mla_in(x, g0, w, gq, gkv, tab, *, tm):
    r, d = x.shape
    ql, kl = gq.shape[1], gkv.shape[1]
    nt = tab.shape[0] // tm
    row = lambda i: (i, 0)
    return pl.pallas_call(
        functools.partial(_mla_in_kernel, ql=ql, kl=kl),
        grid=(r // tm,),
        in_specs=[pl.BlockSpec((tm, d), row), _resident(g0.shape), _resident(w.shape),
                  _resident(gq.shape), _resident(gkv.shape),
                  pl.BlockSpec((tm, LANES), lambda i: (i % nt, 0))],
        out_specs=[pl.BlockSpec((tm, ql), row), pl.BlockSpec((tm, kl), row), pl.BlockSpec((tm, kl), row),
                   pl.BlockSpec((tm, A_ROPE), row), pl.BlockSpec((tm, LANES), row)],
        out_shape=[jax.ShapeDtypeStruct((r, ql), BF16), jax.ShapeDtypeStruct((r, kl), F32),
                   jax.ShapeDtypeStruct((r, kl), BF16), jax.ShapeDtypeStruct((r, A_ROPE), F32),
                   jax.ShapeDtypeStruct((r, LANES), BF16)],
        compiler_params=_params("parallel"),
        name="mla_in",
    )(x, g0, w, gq, gkv, tab)


def _mla_q_kernel(cq_ref, w_ref, tab_ref, qn_ref, qp_ref, *, hn, scale):
    a = _dot(cq_ref[...], w_ref[...])
    qn_ref[...] = (a[:, :hn] * scale).astype(BF16)
    reps = hn // LANES
    c = jnp.concatenate([tab_ref[:, :LANES]] * reps, axis=1)
    s = jnp.concatenate([tab_ref[:, LANES:]] * reps, axis=1)
    qp_ref[...] = ((a[:, hn:2 * hn] * c + a[:, 2 * hn:] * s) * scale).astype(BF16)


def _mla_q(cq, w, tab, *, tm, scale):
    r, ql = cq.shape
    hn = w.shape[1] // 3
    nt = tab.shape[0] // tm
    row = lambda i: (i, 0)
    return pl.pallas_call(
        functools.partial(_mla_q_kernel, hn=hn, scale=scale),
        grid=(r // tm,),
        in_specs=[pl.BlockSpec((tm, ql), row), _resident(w.shape),
                  pl.BlockSpec((tm, 2 * LANES), lambda i: (i % nt, 0))],
        out_specs=[pl.BlockSpec((tm, hn), row), pl.BlockSpec((tm, hn), row)],
        out_shape=[jax.ShapeDtypeStruct((r, hn), BF16), jax.ShapeDtypeStruct((r, hn), BF16)],
        compiler_params=_params("parallel"),
        name="mla_q",
    )(cq, w, tab)


def _mla_kv_kernel(c_ref, wk_ref, wvt_ref, k_ref, vt_ref):
    c = c_ref[...]
    k_ref[...] = _dot(c, wk_ref[...]).astype(BF16)
    vt_ref[0] = _dot_t(wvt_ref[...], c).astype(BF16)


def _mla_kv(ckvb, wk, wvt, *, tm):
    r, kl = ckvb.shape
    hn = wk.shape[1]
    return pl.pallas_call(
        _mla_kv_kernel,
        grid=(r // tm,),
        in_specs=[pl.BlockSpec((tm, kl), lambda i: (i, 0)), _resident(wk.shape), _resident(wvt.shape)],
        out_specs=[pl.BlockSpec((tm, hn), lambda i: (i, 0)), pl.BlockSpec((1, hn, tm), lambda i: (i, 0, 0))],
        out_shape=[jax.ShapeDtypeStruct((r, hn), BF16), jax.ShapeDtypeStruct((r // tm, hn, tm), BF16)],
        compiler_params=_params("parallel"),
        name="mla_kv",
    )(ckvb, wk, wvt)


def _mla_attn_kernel(qn_ref, qp_ref, kn_ref, kp_ref, vt_ref, o_ref, m_s, l_s, acc_s, *, t, hp):
    i = pl.program_id(2)
    m_s[...] = jnp.full(m_s.shape, NEG, F32)
    l_s[...] = jnp.zeros(l_s.shape, F32)
    acc_s[...] = jnp.zeros(acc_s.shape, F32)

    def step(j, masked):
        ks = pl.ds(pl.multiple_of(j * t, t), t)
        kp = kp_ref[ks, :]
        scores = []
        for h in range(hp):
            cols = slice(h * LANES, (h + 1) * LANES)
            q = jnp.concatenate([qn_ref[:, cols], qp_ref[:, cols]], axis=1)
            k = jnp.concatenate([kn_ref[ks, cols], kp], axis=1)
            scores.append(_dot_t(k, q))
        for h in range(hp):
            cols = slice(h * LANES, (h + 1) * LANES)
            s = scores[h]
            if masked:
                kc = lax.broadcasted_iota(jnp.int32, s.shape, 0) // CHUNK
                qc = lax.broadcasted_iota(jnp.int32, s.shape, 1) // CHUNK
                s = jnp.where(kc <= qc, s, NEG)
            m_old = m_s[h]
            m_new = jnp.maximum(m_old, jnp.max(s, axis=0, keepdims=True))
            alpha = jnp.exp2(m_old - m_new)
            p = jnp.exp2(s - m_new)
            l_s[h] = alpha * l_s[h] + jnp.sum(p, axis=0, keepdims=True)
            acc_s[h] = alpha * acc_s[h] + _dot(vt_ref[j, cols, :], p.astype(BF16))
            m_s[h] = m_new

    def body(j, c):
        step(j, False)
        return c

    lax.fori_loop(0, i, body, 0)
    step(i, True)
    for h in range(hp):
        o_ref[:, h * LANES:(h + 1) * LANES] = (acc_s[h] / l_s[h]).T.astype(BF16)


def _mla_attn(qn, qp, kn, kpb, vt, *, nb, s, heads, t, hp):
    r = qn.shape[0]
    nq = s // t
    assert vt.shape[2] == t
    w = hp * LANES
    qmap = lambda b, h, i: (b * nq + i, h)
    return pl.pallas_call(
        functools.partial(_mla_attn_kernel, t=t, hp=hp),
        grid=(nb, heads // hp, nq),
        in_specs=[pl.BlockSpec((t, w), qmap), pl.BlockSpec((t, w), qmap),
                  pl.BlockSpec((s, w), lambda b, h, i: (b, h)), pl.BlockSpec((s, LANES), lambda b, h, i: (b, 0)),
                  pl.BlockSpec((nq, w, t), lambda b, h, i: (b, h, 0))],
        out_specs=pl.BlockSpec((t, w), qmap),
        out_shape=jax.ShapeDtypeStruct((r, heads * A_V), BF16),
        scratch_shapes=[pltpu.VMEM((hp, 1, t), F32), pltpu.VMEM((hp, 1, t), F32), pltpu.VMEM((hp, A_V, t), F32)],
        compiler_params=_params("parallel", "parallel", "arbitrary"),
        name="mla_attn",
    )(qn, qp, kn, kpb, vt)


def _mla_dec_kernel(qn_ref, qp_ref, wuk_ref, wuv_ref, cc_ref, ck_ref, nc_ref, nk_ref, o_ref,
                    ql_s, qpe_s, m_s, l_s, acc_s, *, heads, sq, tk, past):
    j = pl.program_id(1)
    rows = heads * sq

    def online(s, vals):
        m_old = m_s[...]
        m_new = jnp.maximum(m_old, jnp.max(s, axis=1, keepdims=True))
        alpha = jnp.exp2(m_old - m_new)
        p = jnp.exp2(s - m_new)
        l_s[...] = alpha * l_s[...] + jnp.sum(p, axis=1, keepdims=True)
        acc_s[...] = alpha * acc_s[...] + _dot(p.astype(BF16), vals)
        m_s[...] = m_new

    def visible(s, k0):
        qc = (past + lax.broadcasted_iota(jnp.int32, s.shape, 0) % sq) // CHUNK
        kc = (k0 + lax.broadcasted_iota(jnp.int32, s.shape, 1)) // CHUNK
        return jnp.where(kc <= qc, s, NEG)

    @pl.when(j == 0)
    def _():
        for h in range(heads):
            ql_s[h * sq:(h + 1) * sq, :] = _dot(qn_ref[:, h * A_NOPE:(h + 1) * A_NOPE], wuk_ref[h]).astype(BF16)
            qpe_s[h * sq:(h + 1) * sq, :] = qp_ref[:, h * LANES:(h + 1) * LANES]
        m_s[...] = jnp.full(m_s.shape, NEG, F32)
        l_s[...] = jnp.zeros(l_s.shape, F32)
        acc_s[...] = jnp.zeros(acc_s.shape, F32)
        nc = nc_ref[...]
        s = _dot_t(ql_s[...], nc) + _dot_t(qpe_s[...], nk_ref[...])
        online(visible(s, past), nc)

    cc = cc_ref[0].astype(BF16)
    s = _dot_t(ql_s[...], cc) + _dot_t(qpe_s[...], ck_ref[0])
    online(visible(s, j * tk), cc)

    @pl.when(j == pl.num_programs(1) - 1)
    def _():
        ol = (acc_s[...] / l_s[...]).astype(BF16)
        for h in range(heads):
            o_ref[:, h * A_V:(h + 1) * A_V] = _dot(ol[h * sq:(h + 1) * sq, :], wuv_ref[h]).astype(BF16)


def _mla_dec(qn, qp, wukt, wuv, cache_c, cache_kp, ckvb, kpeb, *, sq, tk):
    nb, past, kl = cache_c.shape
    heads = wukt.shape[0]
    rows = heads * sq
    seq = lambda b, j: (b, 0)
    return pl.pallas_call(
        functools.partial(_mla_dec_kernel, heads=heads, sq=sq, tk=tk, past=past),
        grid=(nb, past // tk),
        in_specs=[pl.BlockSpec((sq, heads * A_NOPE), seq), pl.BlockSpec((sq, heads * LANES), seq),
                  _resident(wukt.shape), _resident(wuv.shape),
                  pl.BlockSpec((1, tk, kl), lambda b, j: (b, j, 0)),
                  pl.BlockSpec((1, tk, LANES), lambda b, j: (b, j, 0)),
                  pl.BlockSpec((sq, kl), seq), pl.BlockSpec((sq, LANES), seq)],
        out_specs=pl.BlockSpec((sq, heads * A_V), seq),
        out_shape=jax.ShapeDtypeStruct((nb * sq, heads * A_V), BF16),
        scratch_shapes=[pltpu.VMEM((rows, kl), BF16), pltpu.VMEM((rows, LANES), BF16),
                        pltpu.VMEM((rows, 1), F32), pltpu.VMEM((rows, 1), F32), pltpu.VMEM((rows, kl), F32)],
        compiler_params=_params("parallel", "arbitrary"),
        name="mla_dec",
    )(qn, qp, wukt, wuv, cache_c, cache_kp, ckvb, kpeb)


def _swa_in_kernel(x_ref, g0_ref, w_ref, tab_ref, q_ref, kd_ref, vd_ref, kdb_ref, vdb_ref, *, nq, nk, scale):
    h = _rms(x_ref[...], g0_ref[...]).astype(BF16)
    a = _dot(h, w_ref[...])
    qk = a[:, :nq + nk]
    n = nq + nk
    reps = n // LANES
    c = jnp.concatenate([tab_ref[:, :LANES]] * reps, axis=1)
    s = jnp.concatenate([tab_ref[:, LANES:]] * reps, axis=1)
    half = B_ROT // 2
    lane = lax.broadcasted_iota(jnp.int32, qk.shape, 1) % B_HD
    swapped = jnp.where(lane < half, pltpu.roll(qk, n - half, axis=1), pltpu.roll(qk, half, axis=1))
    roped = qk * c + swapped * s
    q_ref[...] = (roped[:, :nq] * scale).astype(BF16)
    kd = roped[:, nq:]
    vd = a[:, n:]
    kd_ref[...] = kd
    vd_ref[...] = vd
    kdb_ref[...] = kd.astype(BF16)
    vdb_ref[...] = vd.astype(BF16)


def _swa_in(x, g0, w, tab, *, tm, scale, keep):
    r, d = x.shape
    nq = B_HEADS * B_HD
    nk = 2 * B_KV * B_HD
    nt = tab.shape[0] // tm
    row = lambda i: (i, 0)
    last = lambda i: (i // keep, 0)
    return pl.pallas_call(
        functools.partial(_swa_in_kernel, nq=nq, nk=nk, scale=scale),
        grid=(r // tm,),
        in_specs=[pl.BlockSpec((tm, d), row), _resident(g0.shape), _resident(w.shape),
                  pl.BlockSpec((tm, 2 * LANES), lambda i: (i % nt, 0))],
        out_specs=[pl.BlockSpec((tm, nq), row), pl.BlockSpec((tm, nk), last), pl.BlockSpec((tm, nk), last),
                   pl.BlockSpec((tm, nk), row), pl.BlockSpec((tm, nk), row)],
        out_shape=[jax.ShapeDtypeStruct((r, nq), BF16), jax.ShapeDtypeStruct((r // keep, nk), F32),
                   jax.ShapeDtypeStruct((r // keep, nk), F32), jax.ShapeDtypeStruct((r, nk), BF16),
                   jax.ShapeDtypeStruct((r, nk), BF16)],
        compiler_params=_params("arbitrary"),
        name="swa_in",
    )(x, g0, w, tab)


def _swa_group(q_ref, o_ref, sink_ref, g, k2, v2, mask, tq):
    gq = B_HEADS // B_KV
    lane = lax.broadcasted_iota(jnp.int32, (tq, LANES), 1)
    lo = (lane < B_HD).astype(BF16)
    hi = (lane >= B_HD).astype(BF16)
    base = g * gq * B_HD
    pairs = [q_ref[:, base + p * LANES: base + (p + 1) * LANES] for p in range(gq // 2)]
    lhs = jnp.concatenate([x * sel for x in pairs for sel in (lo, hi)], axis=0)
    s = jnp.where(mask, _dot_t(lhs, k2), NEG)
    sk = jnp.concatenate([jnp.full((tq, 1), sink_ref[g * gq + t], F32) for t in range(gq)], axis=0)
    m = jnp.maximum(jnp.max(s, axis=1, keepdims=True), sk)
    e = jnp.exp(s - m)
    den = jnp.sum(e, axis=1, keepdims=True) + jnp.exp(sk - m)
    o = _dot(e.astype(BF16), v2) / den
    lanef = lane < B_HD
    for p in range(gq // 2):
        pair = jnp.where(lanef, o[2 * p * tq:(2 * p + 1) * tq], o[(2 * p + 1) * tq:(2 * p + 2) * tq])
        o_ref[:, base + p * LANES: base + (p + 1) * LANES] = pair.astype(BF16)


def _swa_attn_kernel(sink_ref, q_ref, kp_ref, kc_ref, vp_ref, vc_ref, o_ref, *, tq):
    i = pl.program_id(1)
    gq = B_HEADS // B_KV
    shape = (gq * tq, 2 * tq)
    qc = (lax.broadcasted_iota(jnp.int32, shape, 0) % tq) // CHUNK
    kc = lax.broadcasted_iota(jnp.int32, shape, 1) // CHUNK - tq // CHUNK
    mask = (kc <= qc) & (kc >= qc - WIN_CHUNKS) & ((i > 0) | (kc >= 0))
    for g in range(B_KV):
        cols = slice(g * LANES, (g + 1) * LANES)
        k2 = jnp.concatenate([kp_ref[:, cols], kc_ref[:, cols]], axis=0)
        v2 = jnp.concatenate([vp_ref[:, cols], vc_ref[:, cols]], axis=0)
        _swa_group(q_ref, o_ref, sink_ref, g, k2, v2, mask, tq)


def _swa_attn(sinks, q, kdb, vdb, *, nb, s):
    tq = WIN_CHUNKS * CHUNK
    r, nq = q.shape
    nk = kdb.shape[1]
    nblk = s // tq
    cur = lambda b, i: (b * nblk + i, 0)
    prev = lambda b, i: (b * nblk + jnp.maximum(i - 1, 0), 0)
    return pl.pallas_call(
        functools.partial(_swa_attn_kernel, tq=tq),
        grid=(nb, nblk),
        in_specs=[pl.BlockSpec(memory_space=pltpu.SMEM),
                  pl.BlockSpec((tq, nq), cur), pl.BlockSpec((tq, nk), prev), pl.BlockSpec((tq, nk), cur),
                  pl.BlockSpec((tq, nk), prev), pl.BlockSpec((tq, nk), cur)],
        out_specs=pl.BlockSpec((tq, nq), cur),
        out_shape=jax.ShapeDtypeStruct((r, nq), BF16),
        compiler_params=_params("parallel", "parallel"),
        name="swa_attn",
    )(sinks, q, kdb, kdb, vdb, vdb)


def _swa_dec_kernel(sink_ref, q_ref, k_ref, v_ref, o_ref, *, sq, nkeys, kpad, past):
    gq = B_HEADS // B_KV
    shape = (gq * sq, kpad)
    qc = (past + lax.broadcasted_iota(jnp.int32, shape, 0) % sq) // CHUNK
    col = lax.broadcasted_iota(jnp.int32, shape, 1)
    kc = (past + sq - nkeys + col) // CHUNK
    mask = (kc <= qc) & (kc >= qc - WIN_CHUNKS) & (col < nkeys)
    for g in range(B_KV):
        cols = slice(g * LANES, (g + 1) * LANES)
        _swa_group(q_ref, o_ref, sink_ref, g, k_ref[0, :, cols], v_ref[0, :, cols], mask, sq)


def _swa_dec(sinks, q, k_all, v_all, *, sq, nkeys, past):
    nb, kpad, nk = k_all.shape
    nq = q.shape[1]
    return pl.pallas_call(
        functools.partial(_swa_dec_kernel, sq=sq, nkeys=nkeys, kpad=kpad, past=past),
        grid=(nb,),
        in_specs=[pl.BlockSpec(memory_space=pltpu.SMEM),
                  pl.BlockSpec((sq, nq), lambda b: (b, 0)),
                  pl.BlockSpec((1, kpad, nk), lambda b: (b, 0, 0)),
                  pl.BlockSpec((1, kpad, nk), lambda b: (b, 0, 0))],
        out_specs=pl.BlockSpec((sq, nq), lambda b: (b, 0)),
        out_shape=jax.ShapeDtypeStruct((nb * sq, nq), BF16),
        compiler_params=_params("parallel"),
        name="swa_dec",
    )(sinks, q, k_all, v_all)


def _mix_out_kernel(o_ref, w_ref, x_ref, g1_ref, g2_ref, x1_ref, h2_ref):
    mp = _dot(o_ref[...], w_ref[...])
    x1 = x_ref[...] + _rms(mp, g1_ref[...])
    x1_ref[...] = x1
    h2_ref[...] = _rms(x1, g2_ref[...]).astype(BF16)


def _mix_out(o, w, x, g1, g2, *, tm):
    r, d = x.shape
    row = lambda i: (i, 0)
    return pl.pallas_call(
        _mix_out_kernel,
        grid=(r // tm,),
        in_specs=[pl.BlockSpec((tm, o.shape[1]), row), _resident(w.shape), pl.BlockSpec((tm, d), row),
                  _resident(g1.shape), _resident(g2.shape)],
        out_specs=[pl.BlockSpec((tm, d), row), pl.BlockSpec((tm, d), row)],
        out_shape=[jax.ShapeDtypeStruct((r, d), F32), jax.ShapeDtypeStruct((r, d), BF16)],
        compiler_params=_params("parallel"),
        name="mix_out",
    )(o, w, x, g1, g2)


def _ffn_tail(conv_in, u, cw_ref, cb_ref, wd_ref, x_ref, g3_ref, o_ref, acc_ref):
    g, g1, g2 = conv_in
    j = pl.program_id(1)
    cw = cw_ref[...]
    conv = cb_ref[...] + cw[0:1] * g2 + cw[1:2] * g1 + cw[2:3] * g
    act = conv / (1.0 + jnp.exp(-conv)) * u
    contrib = _dot(act.astype(BF16), wd_ref[...])

    @pl.when(j == 0)
    def _():
        acc_ref[...] = contrib

    @pl.when(j > 0)
    def _():
        acc_ref[...] += contrib

    @pl.when(j == pl.num_programs(1) - 1)
    def _():
        o_ref[...] = x_ref[...] + _rms(acc_ref[...], g3_ref[...])


def _ffn_prompt_kernel(h_ref, x_ref, wg_ref, wu_ref, cw_ref, cb_ref, wd_ref, g3_ref,
                       o_ref, st_ref, acc_ref, carry_ref, *, blocks_per_seq):
    i = pl.program_id(0)
    j = pl.program_id(1)
    h = h_ref[...]
    tm = h.shape[0]
    g = _dot(h, wg_ref[...])
    u = _dot(h, wu_ref[...])

    @pl.when(i % blocks_per_seq == 0)
    def _():
        carry_ref[j] = jnp.zeros(carry_ref.shape[1:], F32)

    prev = carry_ref[j]
    carry_ref[j] = g[tm - 8:, :]
    st_ref[0] = g[tm - (CONV_W - 1):, :]
    row = lax.broadcasted_iota(jnp.int32, g.shape, 0)
    g1 = jnp.where(row == 0, prev[7:8, :], pltpu.roll(g, 1, axis=0))
    g2 = jnp.where(row == 0, prev[6:7, :], jnp.where(row == 1, prev[7:8, :], pltpu.roll(g, 2, axis=0)))
    _ffn_tail((g, g1, g2), u, cw_ref, cb_ref, wd_ref, x_ref, g3_ref, o_ref, acc_ref)


def _ffn_sample_kernel(h_ref, x_ref, wg_ref, wu_ref, cw_ref, cb_ref, wd_ref, g3_ref, i1_ref, i2_ref,
                       o_ref, gate_ref, acc_ref, *, sq):
    h = h_ref[...]
    g = _dot(h, wg_ref[...])
    u = _dot(h, wu_ref[...])
    gate_ref[...] = g
    pos = lax.broadcasted_iota(jnp.int32, g.shape, 0) % sq
    g1 = jnp.where(pos == 0, i1_ref[...], pltpu.roll(g, 1, axis=0))
    g2 = jnp.where(pos < 2, i2_ref[...], pltpu.roll(g, 2, axis=0))
    _ffn_tail((g, g1, g2), u, cw_ref, cb_ref, wd_ref, x_ref, g3_ref, o_ref, acc_ref)


def _ffn_specs(tm, tf, d, nf):
    row = lambda i, j: (i, 0)
    return [pl.BlockSpec((tm, d), row), pl.BlockSpec((tm, d), row),
            pl.BlockSpec((d, tf), lambda i, j: (0, j)), pl.BlockSpec((d, tf), lambda i, j: (0, j + nf)),
            pl.BlockSpec((CONV_W, tf), lambda i, j: (0, j)), pl.BlockSpec((1, tf), lambda i, j: (0, j)),
            pl.BlockSpec((tf, d), lambda i, j: (j, 0)), pl.BlockSpec((1, d), lambda i, j: (0, 0))]


def _ffn_prompt(h2, x1, w_in, cw, cb, wd, g3, *, nb, s, tm, tf):
    r, d = x1.shape
    dff = wd.shape[0]
    nf = dff // tf
    bps = s // tm
    return pl.pallas_call(
        functools.partial(_ffn_prompt_kernel, blocks_per_seq=bps),
        grid=(r // tm, nf),
        in_specs=_ffn_specs(tm, tf, d, nf),
        out_specs=[pl.BlockSpec((tm, d), lambda i, j: (i, 0)),
                   pl.BlockSpec((1, CONV_W - 1, tf), lambda i, j: (i, 0, j))],
        out_shape=[jax.ShapeDtypeStruct((r, d), F32), jax.ShapeDtypeStruct((r // tm, CONV_W - 1, dff), F32)],
        scratch_shapes=[pltpu.VMEM((tm, d), F32), pltpu.VMEM((nf, 8, tf), F32)],
        compiler_params=_params("arbitrary", "arbitrary"),
        name="ffn_prompt",
    )(h2, x1, w_in, w_in, cw, cb, wd, g3)


def _ffn_sample(h2, x1, w_in, cw, cb, wd, g3, inj1, inj2, *, sq, tm, tf):
    r, d = x1.shape
    dff = wd.shape[0]
    nf = dff // tf
    tile = lambda i, j: (i, j)
    return pl.pallas_call(
        functools.partial(_ffn_sample_kernel, sq=sq),
        grid=(r // tm, nf),
        in_specs=_ffn_specs(tm, tf, d, nf) + [pl.BlockSpec((tm, tf), tile), pl.BlockSpec((tm, tf), tile)],
        out_specs=[pl.BlockSpec((tm, d), lambda i, j: (i, 0)), pl.BlockSpec((tm, tf), tile)],
        out_shape=[jax.ShapeDtypeStruct((r, d), F32), jax.ShapeDtypeStruct((r, dff), F32)],
        scratch_shapes=[pltpu.VMEM((tm, d), F32)],
        compiler_params=_params("parallel", "arbitrary"),
        name="ffn_sample",
    )(h2, x1, w_in, w_in, cw, cb, wd, g3, inj1, inj2)


def _rope_cs(pos, rot):
    half = rot // 2
    inv = ROPE_THETA ** (-jnp.arange(half, dtype=F32) / half)
    ang = pos.astype(F32)[:, None] * inv[None, :]
    return jnp.cos(ang), jnp.sin(ang)


def _mla_tables(pos):
    c, s = _rope_cs(pos, A_ROPE)
    cc = jnp.concatenate([c, c], axis=1)
    ss = jnp.concatenate([-s, s], axis=1)
    tab_k = jnp.concatenate([cc, ss], axis=1)
    tab_q = jnp.concatenate([cc, cc, ss, ss], axis=1)
    return tab_k, tab_q


def _swa_table(pos):
    c, s = _rope_cs(pos, B_ROT)
    p = pos.shape[0]
    rest = B_HD - B_ROT
    cc = jnp.concatenate([c, c, jnp.ones((p, rest), F32)], axis=1)
    ss = jnp.concatenate([-s, s, jnp.zeros((p, rest), F32)], axis=1)
    return jnp.concatenate([cc, cc, ss, ss], axis=1)


def _swap_halves(w):
    h = w.shape[-1] // 2
    return jnp.concatenate([w[..., h:], w[..., :h]], axis=-1)


def _row_tile(r, want):
    t = min(r, want)
    assert r % t == 0
    return t


def kernel(x_prompt, x_sample, cache_ckv, cache_kpe, cache_win_k, cache_win_v, state_ffn_conv, norm_g, mla_w_in, mla_g_q, mla_w_qb, mla_g_kv, mla_w_uk, mla_w_uv, mla_w_o, swa_w_qkv, swa_sinks, swa_w_o, ffn_w_in, ffn_conv_w, ffn_conv_b, ffn_w_down):
    nb, s, d = x_prompt.shape
    db, sq, _ = x_sample.shape
    depth = norm_g.shape[0]
    past = cache_ckv.shape[2]
    win = cache_win_k.shape[2]
    ql, kl = mla_g_q.shape[1], mla_g_kv.shape[1]
    heads = mla_w_uk.shape[2]
    dff = ffn_w_down.shape[1]
    gq = B_HEADS // B_KV
    assert mla_w_uk.shape[3] == A_NOPE and mla_w_uv.shape[3] == A_V
    assert mla_w_in.shape[2] == ql + kl + A_ROPE and sq >= CONV_W - 1 and s >= WIN_CHUNKS * CHUNK
    assert A_NOPE == LANES and A_V == LANES
    a_scale = (A_NOPE + A_ROPE) ** -0.5 * LOG2_E
    b_scale = B_HD ** -0.5

    xp = x_prompt.reshape(nb * s, d)
    xs = x_sample.reshape(db * sq, d)
    pos_p = jnp.arange(s, dtype=jnp.int32)
    pos_s = jnp.tile(past + jnp.arange(sq, dtype=jnp.int32), db)
    tabk_p, tabq_p = _mla_tables(pos_p)
    tabk_s, tabq_s = _mla_tables(pos_s)
    tabb_p = _swa_table(pos_p)
    tabb_s = _swa_table(pos_s)

    tm_p = _row_tile(nb * s, 512)
    tm_s = _row_tile(db * sq, 512)
    tf = 512
    assert s % tm_p == 0 and dff % tf == 0
    t_attn = 512
    tk_dec = _row_tile(past, 1024)

    outs = {k: [] for k in ("ckv_p", "kpe_p", "wk_p", "wv_p", "fc_p", "ckv_s", "kpe_s", "wk_s", "wv_s", "fc_s")}
    for i in range(depth):
        g = norm_g[i][:, None, :]
        if i % 2 == 0:
            a = i // 2
            w_in = mla_w_in[a]
            w1 = jnp.concatenate([w_in, _swap_halves(w_in[:, ql + kl:])], axis=1).astype(BF16)
            wq = mla_w_qb[a].reshape(ql, heads, A_NOPE + A_ROPE)
            pad = ((0, 0), (0, 0), (0, LANES - A_ROPE))
            w2 = jnp.concatenate([wq[:, :, :A_NOPE].reshape(ql, heads * A_NOPE),
                                  jnp.pad(wq[:, :, A_NOPE:], pad).reshape(ql, heads * LANES),
                                  jnp.pad(_swap_halves(wq[:, :, A_NOPE:]), pad).reshape(ql, heads * LANES)],
                                 axis=1).astype(BF16)
            wk = mla_w_uk[a].reshape(kl, heads * A_NOPE).astype(BF16)
            wvt = mla_w_uv[a].reshape(kl, heads * A_V).T.astype(BF16)
            wukt = mla_w_uk[a].transpose(1, 2, 0).astype(BF16)
            wuv = mla_w_uv[a].transpose(1, 0, 2).astype(BF16)
            wo = mla_w_o[a].astype(BF16)
            gq_, gkv_ = mla_g_q[a][None, :], mla_g_kv[a][None, :]

            cq, ckv, ckvb, kpe, kpeb = _mla_in(xp, g[0], w1, gq_, gkv_, tabk_p, tm=tm_p)
            qn, qp = _mla_q(cq, w2, tabq_p, tm=_row_tile(nb * s, 256), scale=a_scale)
            kn, vt = _mla_kv(ckvb, wk, wvt, tm=min(t_attn, s))
            op = _mla_attn(qn, qp, kn, kpeb, vt, nb=nb, s=s, heads=heads, t=min(t_attn, s), hp=4)
            outs["ckv_p"].append(ckv.reshape(nb, s, kl))
            outs["kpe_p"].append(kpe.reshape(nb, s, A_ROPE))

            cq, ckv, ckvb, kpe, kpeb = _mla_in(xs, g[0], w1, gq_, gkv_, tabk_s, tm=tm_s)
            qn, qp = _mla_q(cq, w2, tabq_s, tm=_row_tile(db * sq, 256), scale=a_scale)
            cache_kp = jnp.pad(cache_kpe[a].astype(BF16), ((0, 0), (0, 0), (0, LANES - A_ROPE)))
            os_ = _mla_dec(qn, qp, wukt, wuv, cache_ckv[a], cache_kp, ckvb, kpeb, sq=sq, tk=tk_dec)
            outs["ckv_s"].append(ckv.reshape(db, sq, kl))
            outs["kpe_s"].append(kpe.reshape(db, sq, A_ROPE))
        else:
            jb = i // 2
            wqkv = swa_w_qkv[jb]
            nq = B_HEADS * B_HD
            nkv = B_KV * B_HD
            dup = lambda w: jnp.concatenate([w.reshape(d, B_KV, B_HD)] * 2, axis=2).reshape(d, 2 * nkv)
            w3 = jnp.concatenate([wqkv[:, :nq], dup(wqkv[:, nq:nq + nkv]), dup(wqkv[:, nq + nkv:])],
                                 axis=1).astype(BF16)
            wo = swa_w_o[jb].astype(BF16)
            sinks = swa_sinks[jb].astype(F32)
            undup = lambda t, n, rows: t.reshape(n, rows, B_KV, 2, B_HD)[:, :, :, 0, :]

            tm_b = _row_tile(s, 256)
            kw = min(WIN_CHUNKS * CHUNK, s)
            assert kw <= tm_b
            q, kd, vd, kdb, vdb = _swa_in(xp, g[0], w3, tabb_p, tm=tm_b, scale=b_scale, keep=s // tm_b)
            op = _swa_attn(sinks, q, kdb, vdb, nb=nb, s=s)
            outs["wk_p"].append(undup(kd, nb, tm_b)[:, tm_b - kw:])
            outs["wv_p"].append(undup(vd, nb, tm_b)[:, tm_b - kw:])

            q, kd, vd, kdb, vdb = _swa_in(xs, g[0], w3, tabb_s, tm=_row_tile(db * sq, 256), scale=b_scale, keep=1)
            nkeys = win + sq
            kpad = -(-nkeys // LANES) * LANES
            dup_c = lambda c: jnp.concatenate([c, c], axis=3).reshape(db, win, 2 * nkv).astype(BF16)
            cat = lambda c, new: jnp.pad(jnp.concatenate([dup_c(c), new.reshape(db, sq, 2 * nkv)], axis=1),
                                         ((0, 0), (0, kpad - nkeys), (0, 0)))
            os_ = _swa_dec(sinks, q, cat(cache_win_k[jb], kdb), cat(cache_win_v[jb], vdb),
                           sq=sq, nkeys=nkeys, past=past)
            outs["wk_s"].append(jnp.concatenate([cache_win_k[jb], undup(kd, db, sq)], axis=1)[:, -win:])
            outs["wv_s"].append(jnp.concatenate([cache_win_v[jb], undup(vd, db, sq)], axis=1)[:, -win:])

        x1p, h2p = _mix_out(op, wo, xp, g[1], g[2], tm=_row_tile(nb * s, 256))
        x1s, h2s = _mix_out(os_, wo, xs, g[1], g[2], tm=_row_tile(db * sq, 256))

        fw_in = ffn_w_in[i].astype(BF16)
        fw_d = ffn_w_down[i].astype(BF16)
        cw, cb = ffn_conv_w[i], ffn_conv_b[i][None, :]
        xp, fc = _ffn_prompt(h2p, x1p, fw_in, cw, cb, fw_d, g[3], nb=nb, s=s, tm=tm_p, tf=tf)
        outs["fc_p"].append(fc[s // tm_p - 1::s // tm_p])
        st = state_ffn_conv[i]
        zeros = jnp.zeros((db, sq, dff), F32)
        inj1 = zeros.at[:, 0].set(st[:, 1]).reshape(db * sq, dff)
        inj2 = zeros.at[:, 0].set(st[:, 0]).at[:, 1].set(st[:, 1]).reshape(db * sq, dff)
        xs, gate = _ffn_sample(h2s, x1s, fw_in, cw, cb, fw_d, g[3], inj1, inj2, sq=sq, tm=tm_s, tf=tf)
        outs["fc_s"].append(gate.reshape(db, sq, dff)[:, sq - (CONV_W - 1):])

    st = lambda k: jnp.stack(outs[k])
    return (xp.reshape(nb, s, d), xs.reshape(db, sq, d),
            st("ckv_p"), st("kpe_p"), st("wk_p"), st("wv_p"), st("fc_p"),
            st("ckv_s"), st("kpe_s"), st("wk_s"), st("wv_s"), st("fc_s"))
```
